```python
import math
import jax, jax.numpy as jnp
from jax import lax
import numpy as np

D_MODEL = 1024
BATCH = 8
SEQ = 8192
DEPTH = 1

N_META = 16
D_MIX = D_MODEL
D_LRU = D_MIX // 2
D_CC = D_MIX - D_LRU
LRU_HEADS = 8
LRU_HEAD_DIM = D_LRU // LRU_HEADS
CC_GROUPS = 8
LRU_CONV_W = 4
CC_CONV_W = 31
LRU_C = 8.0
D_FF = 4 * D_MODEL
EPS = 1e-6

kernel_name = "hymba_style_rglru_conformer_conv_hybrid"


def rmsnorm(x, g):
    xf = x.astype(jnp.float32)
    y = xf * lax.rsqrt(jnp.mean(xf * xf, axis=-1, keepdims=True) + EPS)
    return (y * g.astype(jnp.float32)).astype(x.dtype)


def layernorm(x, g, b):
    xf = x.astype(jnp.float32)
    mu = jnp.mean(xf, axis=-1, keepdims=True)
    var = jnp.mean(jnp.square(xf - mu), axis=-1, keepdims=True)
    y = (xf - mu) * lax.rsqrt(var + EPS)
    return (y * g.astype(jnp.float32) + b.astype(jnp.float32)).astype(x.dtype)


def causal_depthwise_conv(x, w, b):
    k = w.shape[0]
    y = lax.conv_general_dilated(
        x, w[:, None, :].astype(x.dtype), window_strides=(1,), padding=[(k - 1, 0)],
        dimension_numbers=("NWC", "WIO", "NWC"), feature_group_count=x.shape[-1])
    return y + b.astype(x.dtype)


def block_diag_linear(x, w, b):
    B, T, _ = x.shape
    h, dh, _ = w.shape
    y = jnp.einsum("bthi,hij->bthj", x.reshape(B, T, h, dh), w.astype(x.dtype))
    return y.reshape(B, T, h * dh) + b.astype(x.dtype)


def rg_lru(x, w_a, b_a, w_x, b_x, a_param):
    xf = x.astype(jnp.float32)
    r = jax.nn.sigmoid(block_diag_linear(xf, w_a, b_a))
    i = jax.nn.sigmoid(block_diag_linear(xf, w_x, b_x))
    log_a = -LRU_C * r * jax.nn.softplus(-a_param.astype(jnp.float32))
    a = jnp.exp(log_a)
    mult = jnp.sqrt(-jnp.expm1(2.0 * log_a))
    bterm = mult * (i * xf)

    def combine(left, right):
        a1, b1 = left
        a2, b2 = right
        return a1 * a2, a2 * b1 + b2

    _, h = lax.associative_scan(combine, (a, bterm), axis=1)
    return h.astype(x.dtype)


def setup_inputs(seed: int = 0) -> dict:
    key = jax.random.key(seed)
    ks = jax.random.split(key, 24)
    f32 = jnp.float32

    def nrm(k, shape, scale):
        return jax.random.normal(k, shape, f32) * scale

    x = jax.random.normal(ks[0], (BATCH, SEQ, D_MODEL), f32)
    meta_tokens = nrm(ks[1], (N_META, D_MODEL), 1.0)
    norm_mix_g = 1.0 + nrm(ks[2], (DEPTH, D_MODEL), 0.02)
    w_in = nrm(ks[3], (DEPTH, D_MODEL, 2 * D_LRU + 2 * D_CC), D_MODEL ** -0.5)
    lru_conv_w = nrm(ks[4], (DEPTH, LRU_CONV_W, D_LRU), LRU_CONV_W ** -0.5)
    lru_conv_b = nrm(ks[5], (DEPTH, D_LRU), 0.01)
    lru_gate_a_w = nrm(ks[6], (DEPTH, LRU_HEADS, LRU_HEAD_DIM, LRU_HEAD_DIM), LRU_HEAD_DIM ** -0.5)
    lru_gate_a_b = nrm(ks[7], (DEPTH, D_LRU), 0.01)
    lru_gate_x_w = nrm(ks[8], (DEPTH, LRU_HEADS, LRU_HEAD_DIM, LRU_HEAD_DIM), LRU_HEAD_DIM ** -0.5)
    lru_gate_x_b = nrm(ks[9], (DEPTH, D_LRU), 0.01)
    a_c = jax.random.uniform(ks[10], (DEPTH, D_LRU), f32, minval=0.9, maxval=0.999)
    a_base = jnp.power(a_c, 1.0 / LRU_C)
    lru_a_param = jnp.log(a_base) - jnp.log1p(-a_base)
    cc_conv_w = nrm(ks[11], (DEPTH, CC_CONV_W, D_CC), CC_CONV_W ** -0.5)
    cc_conv_b = nrm(ks[12], (DEPTH, D_CC), 0.01)
    cc_ln_g = 1.0 + nrm(ks[13], (DEPTH, D_CC), 0.02)
    cc_ln_b = nrm(ks[14], (DEPTH, D_CC), 0.01)
    out_norm_lru_g = 1.0 + nrm(ks[15], (DEPTH, D_LRU), 0.02)
    out_norm_cc_g = 1.0 + nrm(ks[16], (DEPTH, D_CC), 0.02)
    w_out = nrm(ks[17], (DEPTH, D_MIX, D_MODEL), D_MIX ** -0.5)
    norm_mlp_g = 1.0 + nrm(ks[18], (DEPTH, D_MODEL), 0.02)
    w_mlp_up = nrm(ks[19], (DEPTH, D_MODEL, D_FF), D_MODEL ** -0.5)
    w_mlp_down = nrm(ks[20], (DEPTH, D_FF, D_MODEL), D_FF ** -0.5)
    norm_final_g = 1.0 + nrm(ks[21], (D_MODEL,), 0.02)
    return {
        "x": x, "meta_tokens": meta_tokens, "norm_mix_g": norm_mix_g, "w_in": w_in,
        "lru_conv_w": lru_conv_w, "lru_conv_b": lru_conv_b,
        "lru_gate_a_w": lru_gate_a_w, "lru_gate_a_b": lru_gate_a_b,
        "lru_gate_x_w": lru_gate_x_w, "lru_gate_x_b": lru_gate_x_b,
        "lru_a_param": lru_a_param, "cc_conv_w": cc_conv_w, "cc_conv_b": cc_conv_b,
        "cc_ln_g": cc_ln_g, "cc_ln_b": cc_ln_b,
        "out_norm_lru_g": out_norm_lru_g, "out_norm_cc_g": out_norm_cc_g, "w_out": w_out,
        "norm_mlp_g": norm_mlp_g, "w_mlp_up": w_mlp_up, "w_mlp_down": w_mlp_down,
        "norm_final_g": norm_final_g,
    }


def reference(x, meta_tokens, norm_mix_g, w_in, lru_conv_w, lru_conv_b, lru_gate_a_w,
              lru_gate_a_b, lru_gate_x_w, lru_gate_x_b, lru_a_param, cc_conv_w, cc_conv_b,
              cc_ln_g, cc_ln_b, out_norm_lru_g, out_norm_cc_g, w_out, norm_mlp_g,
              w_mlp_up, w_mlp_down, norm_final_g):
    B = x.shape[0]
    meta = jnp.broadcast_to(meta_tokens[None].astype(x.dtype), (B, N_META, D_MODEL))
    h = jnp.concatenate([meta, x], axis=1)

    for l in range(DEPTH):
        u = rmsnorm(h, norm_mix_g[l])
        proj = u @ w_in[l].astype(u.dtype)
        x_lru, g_lru, v_cc, g_cc = jnp.split(
            proj, [D_LRU, 2 * D_LRU, 2 * D_LRU + D_CC], axis=-1)

        xc = causal_depthwise_conv(x_lru, lru_conv_w[l], lru_conv_b[l])
        y_lru = rg_lru(xc, lru_gate_a_w[l], lru_gate_a_b[l], lru_gate_x_w[l],
                       lru_gate_x_b[l], lru_a_param[l]) * jax.nn.gelu(g_lru)

        c = v_cc * jax.nn.sigmoid(g_cc)
        c = causal_depthwise_conv(c, cc_conv_w[l], cc_conv_b[l])
        y_cc = jax.nn.silu(layernorm(c, cc_ln_g[l], cc_ln_b[l]))

        y = jnp.concatenate([rmsnorm(y_lru, out_norm_lru_g[l]),
                             rmsnorm(y_cc, out_norm_cc_g[l])], axis=-1)
        h = h + y @ w_out[l].astype(y.dtype)

        m = rmsnorm(h, norm_mlp_g[l]) @ w_mlp_up[l].astype(h.dtype)
        h = h + jnp.square(jax.nn.relu(m)) @ w_mlp_down[l].astype(h.dtype)

    out = rmsnorm(h, norm_final_g)
    return out[:, N_META:]
```

```python
import functools

import jax
import jax.numpy as jnp
from jax import lax
from jax.experimental import pallas as pl
from jax.experimental.pallas import tpu as pltpu

D_MODEL = 1024
N_META = 16
D_LRU = 512
D_CC = 512
LRU_HEADS = 8
LRU_HEAD_DIM = 64
LRU_CONV_W = 4
CC_CONV_W = 31
LRU_C = 8.0
D_FF = 4096
EPS = 1e-6

SUBLANES = 8
LANES = 128
MXU_DIM = 256
N_SLAB = D_LRU // LANES
GATE_GROUPS = D_LRU // MXU_DIM

TM = 512
SCAN_CHUNKS = SUBLANES
SCAN_LEN = TM // SCAN_CHUNKS
SCAN_PITCH = SCAN_LEN + SUBLANES
X_HALO = 8
C_HALO = 32
CONV_ROWS = 64

TM_MLP = 512
FF_CHUNK = 1024

VMEM_LIMIT = 52 * 1024 * 1024


def _rmsnorm(x, g):
    return x * lax.rsqrt(jnp.mean(x * x, axis=-1, keepdims=True) + EPS) * g


def _dot(a, b):
    return jnp.dot(a, b, preferred_element_type=jnp.float32)


def _slab(s):
    return slice(s * LANES, (s + 1) * LANES)


def _dwconv(buf_ref, w_ref, b_ref, out_ref, n_rows, taps, halo, chunk):
    base = halo - (taps - 1)
    for s in range(N_SLAB):
        lanes = _slab(s)

        def body(i, carry, s=s, lanes=lanes):
            r0 = pl.multiple_of(i * chunk, SUBLANES)
            acc = jnp.broadcast_to(b_ref[0:1, lanes], (chunk, LANES))
            for k in range(taps):
                acc = acc + w_ref[k:k + 1, lanes] * buf_ref[s, pl.ds(r0 + base + k, chunk), :]
            out_ref[pl.ds(r0, chunk), lanes] = acc
            return carry

        lax.fori_loop(0, n_rows // chunk, body, 0)


def _mixer_front(h, n, chunk, p, xbuf, cbuf, xc_ref):
    u = _rmsnorm(h, p["g_mix"][...]).astype(jnp.bfloat16)
    x_lru = _dot(u, p["w_in"][:, 0:D_LRU])
    for s in range(N_SLAB):
        xbuf[s, X_HALO:X_HALO + n, :] = x_lru[:, _slab(s)]
    _dwconv(xbuf, p["lru_conv_w"], p["lru_conv_b"], xc_ref, n, LRU_CONV_W, X_HALO, chunk)
    xc = xc_ref[0:n, :]
    xcb = xc.astype(jnp.bfloat16)
    pre_a = jnp.concatenate(
        [_dot(xcb[:, g * MXU_DIM:(g + 1) * MXU_DIM], p["wa"][g]) for g in range(GATE_GROUPS)], axis=-1)
    pre_x = jnp.concatenate(
        [_dot(xcb[:, g * MXU_DIM:(g + 1) * MXU_DIM], p["wx"][g]) for g in range(GATE_GROUPS)], axis=-1)
    r = jax.nn.sigmoid(pre_a + p["ba"][...])
    i = jax.nn.sigmoid(pre_x + p["bx"][...])
    log_a = (-LRU_C) * r * jax.nn.softplus(-p["a_param"][...])
    a = jnp.exp(log_a)
    th = jnp.tanh(log_a)
    mult = jnp.sqrt(-2.0 * th / (1.0 - th))
    bterm = mult * (i * xc)

    v_cc = _dot(u, p["w_in"][:, 2 * D_LRU:2 * D_LRU + D_CC])
    g_cc = _dot(u, p["w_in"][:, 2 * D_LRU + D_CC:2 * D_LRU + 2 * D_CC])
    c = v_cc * jax.nn.sigmoid(g_cc)
    for s in range(N_SLAB):
        cbuf[s, C_HALO:C_HALO + n, :] = c[:, _slab(s)]
    return u, a, bterm


def _mixer_kernel(x_ref, meta_ref, g_mix, w_in, lru_conv_w, lru_conv_b, wa, ba, wx, bx, a_param,
                  cc_conv_w, cc_conv_b, ln_g, ln_b, on_lru_g, on_cc_g, w_out,
                  h1_ref,
                  xbuf, cbuf, a_s, b_s, xc_ref, hl_ref, cc_ref, carry_ref, xinit, cinit, hinit):
    p = dict(g_mix=g_mix, w_in=w_in, lru_conv_w=lru_conv_w, lru_conv_b=lru_conv_b,
             wa=wa, ba=ba, wx=wx, bx=bx, a_param=a_param)
    bi = pl.program_id(0)
    ti = pl.program_id(1)

    @pl.when((bi == 0) & (ti == 0))
    def _():
        for s in range(N_SLAB):
            xbuf[s, 0:X_HALO, :] = jnp.zeros((X_HALO, LANES), jnp.float32)
        _, a, bterm = _mixer_front(meta_ref[...], N_META, N_META, p, xbuf, cbuf, xc_ref)
        hrow = jnp.zeros((1, D_LRU), jnp.float32)
        for t in range(N_META):
            hrow = a[t:t + 1, :] * hrow + bterm[t:t + 1, :]
        hinit[...] = jnp.broadcast_to(hrow, (SUBLANES, D_LRU))
        for s in range(N_SLAB):
            xinit[s] = xbuf[s, N_META:N_META + X_HALO, :]
            cinit[s, 0:C_HALO - N_META, :] = jnp.zeros((C_HALO - N_META, LANES), jnp.float32)
            cinit[s, C_HALO - N_META:C_HALO, :] = cbuf[s, C_HALO:C_HALO + N_META, :]

    @pl.when(ti == 0)
    def _():
        for s in range(N_SLAB):
            xbuf[s, 0:X_HALO, :] = xinit[s]
            cbuf[s, 0:C_HALO, :] = cinit[s]
        carry_ref[...] = hinit[...]

    h = x_ref[...]
    u, a, bterm = _mixer_front(h, TM, CONV_ROWS, p, xbuf, cbuf, xc_ref)

    for s in range(N_SLAB):
        for c in range(SCAN_CHUNKS):
            rows = slice(c * SCAN_LEN, (c + 1) * SCAN_LEN)
            dst = slice(c * SCAN_PITCH, c * SCAN_PITCH + SCAN_LEN)
            a_s[s, dst, :] = a[rows, _slab(s)]
            b_s[s, dst, :] = bterm[rows, _slab(s)]

    def scan_body(j, carry):
        hs, ps = carry
        new_h, new_p = [], []
        for s in range(N_SLAB):
            idx = pl.ds(j, SCAN_CHUNKS, stride=SCAN_PITCH)
            av = a_s[s, idx, :]
            bv = b_s[s, idx, :]
            hv = av * hs[s] + bv
            pv = av * ps[s]
            b_s[s, idx, :] = hv
            a_s[s, idx, :] = pv
            new_h.append(hv)
            new_p.append(pv)
        return tuple(new_h), tuple(new_p)

    zeros = tuple(jnp.zeros((SCAN_CHUNKS, LANES), jnp.float32) for _ in range(N_SLAB))
    ones = tuple(jnp.ones((SCAN_CHUNKS, LANES), jnp.float32) for _ in range(N_SLAB))
    h_end, p_end = lax.fori_loop(0, SCAN_LEN, scan_body, (zeros, ones), unroll=4)

    for s in range(N_SLAB):
        cur = carry_ref[0:1, _slab(s)]
        for c in range(SCAN_CHUNKS):
            src = slice(c * SCAN_PITCH, c * SCAN_PITCH + SCAN_LEN)
            rows = slice(c * SCAN_LEN, (c + 1) * SCAN_LEN)
            hl_ref[rows, _slab(s)] = b_s[s, src, :] + a_s[s, src, :] * cur
            cur = p_end[s][c:c + 1, :] * cur + h_end[s][c:c + 1, :]
        carry_ref[:, _slab(s)] = jnp.broadcast_to(cur, (SUBLANES, LANES))

    g_lru = _dot(u, w_in[:, D_LRU:2 * D_LRU])
    y_lru = hl_ref[...] * jax.nn.gelu(g_lru)

    _dwconv(cbuf, cc_conv_w, cc_conv_b, cc_ref, TM, CC_CONV_W, C_HALO, CONV_ROWS)
    cc = cc_ref[...]
    mu = jnp.mean(cc, axis=-1, keepdims=True)
    cen = cc - mu
    var = jnp.mean(cen * cen, axis=-1, keepdims=True)
    ln = cen * lax.rsqrt(var + EPS) * ln_g[...] + ln_b[...]
    y_cc = ln * jax.nn.sigmoid(ln)

    y = jnp.concatenate([_rmsnorm(y_lru, on_lru_g[...]), _rmsnorm(y_cc, on_cc_g[...])], axis=-1)
    h1_ref[...] = h + _dot(y.astype(jnp.bfloat16), w_out[...])

    for s in range(N_SLAB):
        xbuf[s, 0:X_HALO, :] = xbuf[s, TM:TM + X_HALO, :]
        cbuf[s, 0:C_HALO, :] = cbuf[s, TM:TM + C_HALO, :]


def _mlp_kernel(h_ref, g_mlp, w_up, w_down, g_final, out_ref):
    h = h_ref[...]
    m = _rmsnorm(h, g_mlp[...]).astype(jnp.bfloat16)
    acc = h
    for k in range(D_FF // FF_CHUNK):
        cols = slice(k * FF_CHUNK, (k + 1) * FF_CHUNK)
        hid = jnp.maximum(_dot(m, w_up[:, cols]), 0.0)
        acc = acc + _dot((hid * hid).astype(jnp.bfloat16), w_down[cols, :])
    out_ref[...] = _rmsnorm(acc, g_final[...])


def _const_spec(shape):
    zeros = (0,) * len(shape)
    return pl.BlockSpec(shape, lambda *_: zeros, pipeline_mode=pl.Buffered(1))


def _block_diag(w):
    per = MXU_DIM // LRU_HEAD_DIM
    w4 = w.reshape(GATE_GROUPS, per, LRU_HEAD_DIM, LRU_HEAD_DIM)
    eye = jnp.eye(per, dtype=w.dtype)
    bd = jnp.einsum("ghij,hk->ghikj", w4, eye)
    return bd.reshape(GATE_GROUPS, MXU_DIM, MXU_DIM).astype(jnp.bfloat16)


def kernel(x, meta_tokens, norm_mix_g, w_in, lru_conv_w, lru_conv_b, lru_gate_a_w, lru_gate_a_b,
           lru_gate_x_w, lru_gate_x_b, lru_a_param, cc_conv_w, cc_conv_b, cc_ln_g, cc_ln_b,
           out_norm_lru_g, out_norm_cc_g, w_out, norm_mlp_g, w_mlp_up, w_mlp_down, norm_final_g):
    batch, seq, d = x.shape
    assert d == D_MODEL and seq % TM == 0 and (batch * seq) % TM_MLP == 0
    assert w_in.shape[0] == 1, "single layer"
    bf16 = jnp.bfloat16
    row = lambda v: v.reshape(1, -1)

    mixer_inputs = (
        x, meta_tokens, row(norm_mix_g[0]), w_in[0].astype(bf16),
        lru_conv_w[0], row(lru_conv_b[0]),
        _block_diag(lru_gate_a_w[0]), row(lru_gate_a_b[0]),
        _block_diag(lru_gate_x_w[0]), row(lru_gate_x_b[0]),
        row(lru_a_param[0]), cc_conv_w[0], row(cc_conv_b[0]),
        row(cc_ln_g[0]), row(cc_ln_b[0]), row(out_norm_lru_g[0]), row(out_norm_cc_g[0]),
        w_out[0].astype(bf16),
    )
    x_spec = pl.BlockSpec((None, TM, D_MODEL), lambda b, t: (b, t, 0))
    in_specs = [x_spec] + [_const_spec(v.shape) for v in mixer_inputs[1:]]
    f32 = jnp.float32
    h1 = pl.pallas_call(
        _mixer_kernel,
        out_shape=jax.ShapeDtypeStruct((batch, seq, D_MODEL), f32),
        grid=(batch, seq // TM),
        in_specs=in_specs,
        out_specs=pl.BlockSpec((None, TM, D_MODEL), lambda b, t: (b, t, 0)),
        scratch_shapes=[
            pltpu.VMEM((N_SLAB, X_HALO + TM, LANES), f32),
            pltpu.VMEM((N_SLAB, C_HALO + TM, LANES), f32),
            pltpu.VMEM((N_SLAB, SCAN_CHUNKS * SCAN_PITCH, LANES), f32),
            pltpu.VMEM((N_SLAB, SCAN_CHUNKS * SCAN_PITCH, LANES), f32),
            pltpu.VMEM((TM, D_LRU), f32),
            pltpu.VMEM((TM, D_LRU), f32),
            pltpu.VMEM((TM, D_CC), f32),
            pltpu.VMEM((SUBLANES, D_LRU), f32),
            pltpu.VMEM((N_SLAB, X_HALO, LANES), f32),
            pltpu.VMEM((N_SLAB, C_HALO, LANES), f32),
            pltpu.VMEM((SUBLANES, D_LRU), f32),
        ],
        compiler_params=pltpu.CompilerParams(
            dimension_semantics=("arbitrary", "arbitrary"), vmem_limit_bytes=VMEM_LIMIT),
        name="mixer",
    )(*mixer_inputs)

    n_tok = batch * seq
    mlp_inputs = (h1.reshape(n_tok, D_MODEL), row(norm_mlp_g[0]), w_mlp_up[0].astype(bf16),
                  w_mlp_down[0].astype(bf16), row(norm_final_g))
    tok_spec = pl.BlockSpec((TM_MLP, D_MODEL), lambda i: (i, 0))
    out = pl.pallas_call(
        _mlp_kernel,
        out_shape=jax.ShapeDtypeStruct((n_tok, D_MODEL), f32),
        grid=(n_tok // TM_MLP,),
        in_specs=[tok_spec] + [_const_spec(v.shape) for v in mlp_inputs[1:]],
        out_specs=tok_spec,
        compiler_params=pltpu.CompilerParams(
            dimension_semantics=("parallel",), vmem_limit_bytes=VMEM_LIMIT),
        name="mlp",
    )(*mlp_inputs)
    return out.reshape(batch, seq, D_MODEL)
```

```python
import functools

import jax
import jax.numpy as jnp
from jax import lax
from jax.experimental import pallas as pl
from jax.experimental.pallas import tpu as pltpu

D_MODEL = 1024
N_META = 16
D_LRU = 512
D_CC = 512
LRU_HEADS = 8
LRU_HEAD_DIM = 64
LRU_CONV_W = 4
CC_CONV_W = 31
LRU_C = 8.0
D_FF = 4096
EPS = 1e-6

SUBLANES = 8
LANES = 128
MXU_DIM = 256
N_SLAB = D_LRU // LANES
GATE_GROUPS = D_LRU // MXU_DIM

TM = 512
SCAN_CHUNKS = SUBLANES
SCAN_LEN = TM // SCAN_CHUNKS
SCAN_PITCH = SCAN_LEN + SUBLANES
X_HALO = 8
C_HALO = 32
CONV_ROWS = 64
ROW_PIECE = 128
FF_CHUNK = 1024
PIECE_N = 512

VMEM_LIMIT = 58 * 1024 * 1024


def _rmsnorm(x, g):
    return x * lax.rsqrt(jnp.mean(x * x, axis=-1, keepdims=True) + EPS) * g


def _dot(a, b):
    return jnp.dot(a, b, preferred_element_type=jnp.float32)


def _slab(s):
    return slice(s * LANES, (s + 1) * LANES)


def _dwconv_chunk(buf_ref, w_ref, b_ref, out_ref, s, r0, taps, halo, chunk):
    base = halo - (taps - 1)
    lanes = _slab(s)
    acc = jnp.broadcast_to(b_ref[0:1, lanes], (chunk, LANES))
    for k in range(taps):
        acc = acc + w_ref[k:k + 1, lanes] * buf_ref[s, r0 + base + k:r0 + base + k + chunk, :]
    out_ref[r0:r0 + chunk, lanes] = acc


def _lru_coeffs(xc, pre_a, pre_x, p):
    r = jax.nn.sigmoid(pre_a + p["ba"][...])
    i = jax.nn.sigmoid(pre_x + p["bx"][...])
    log_a = (-LRU_C) * r * jax.nn.softplus(-p["a_param"][...])
    a = jnp.exp(log_a)
    th = jnp.tanh(log_a)
    mult = jnp.sqrt(-2.0 * th / (1.0 - th))
    return a, mult * (i * xc)


def _gate_dots(xcb, w):
    return jnp.concatenate(
        [_dot(xcb[:, g * MXU_DIM:(g + 1) * MXU_DIM], w[g]) for g in range(GATE_GROUPS)], axis=-1)


def _meta_state(meta_ref, p, xbuf, cbuf, xc_ref, xinit, cinit, hinit):
    n = N_META
    for s in range(N_SLAB):
        xbuf[s, 0:X_HALO, :] = jnp.zeros((X_HALO, LANES), jnp.float32)
    u = _rmsnorm(meta_ref[...], p["g_mix"][...]).astype(jnp.bfloat16)
    x_lru = _dot(u, p["w_in"][:, 0:D_LRU])
    for s in range(N_SLAB):
        xbuf[s, X_HALO:X_HALO + n, :] = x_lru[:, _slab(s)]
        _dwconv_chunk(xbuf, p["lru_conv_w"], p["lru_conv_b"], xc_ref, s, 0, LRU_CONV_W, X_HALO, n)
    xc = xc_ref[0:n, :]
    xcb = xc.astype(jnp.bfloat16)
    a, bterm = _lru_coeffs(xc, _gate_dots(xcb, p["wa"]), _gate_dots(xcb, p["wx"]), p)
    hrow = jnp.zeros((1, D_LRU), jnp.float32)
    for t in range(n):
        hrow = a[t:t + 1, :] * hrow + bterm[t:t + 1, :]
    hinit[...] = jnp.broadcast_to(hrow, (SUBLANES, D_LRU))
    v_cc = _dot(u, p["w_in"][:, 2 * D_LRU:2 * D_LRU + D_CC])
    g_cc = _dot(u, p["w_in"][:, 2 * D_LRU + D_CC:2 * D_LRU + 2 * D_CC])
    c = v_cc * jax.nn.sigmoid(g_cc)
    for s in range(N_SLAB):
        xinit[s] = xbuf[s, n:n + X_HALO, :]
        cinit[s, 0:C_HALO - n, :] = jnp.zeros((C_HALO - n, LANES), jnp.float32)
        cinit[s, C_HALO - n:C_HALO, :] = c[:, _slab(s)]


def _mixer_pieces(x_ref, p, q, sc):
    pieces = []
    add = lambda cost: (lambda fn: pieces.append((cost, fn)))
    w_in = p["w_in"]
    row_pieces = [slice(r, r + ROW_PIECE) for r in range(0, TM, ROW_PIECE)]

    @add(600)
    def _():
        sc["u"][...] = _rmsnorm(x_ref[...], p["g_mix"][...]).astype(jnp.bfloat16)

    @add(1024)
    def _():
        x_lru = _dot(sc["u"][...], w_in[:, 0:D_LRU])
        for s in range(N_SLAB):
            sc["xbuf"][s, X_HALO:X_HALO + TM, :] = x_lru[:, _slab(s)]

    for s in range(N_SLAB):
        @add(250)
        def _(s=s):
            for r0 in range(0, TM, CONV_ROWS):
                _dwconv_chunk(sc["xbuf"], p["lru_conv_w"], p["lru_conv_b"], sc["xc"], s, r0,
                              LRU_CONV_W, X_HALO, CONV_ROWS)

    @add(2500)
    def _():
        u = sc["u"][...]
        v_cc = _dot(u, w_in[:, 2 * D_LRU:2 * D_LRU + D_CC])
        g_cc = _dot(u, w_in[:, 2 * D_LRU + D_CC:2 * D_LRU + 2 * D_CC])
        c = v_cc * jax.nn.sigmoid(g_cc)
        for s in range(N_SLAB):
            sc["cbuf"][s, C_HALO:C_HALO + TM, :] = c[:, _slab(s)]

    @add(512)
    def _():
        xcb = sc["xc"][...].astype(jnp.bfloat16)
        sc["hl"][...] = _gate_dots(xcb, p["wa"])
        sc["cc"][...] = _gate_dots(xcb, p["wx"])

    for rows in row_pieces:
        @add(625)
        def _(rows=rows):
            a, bterm = _lru_coeffs(sc["xc"][rows, :], sc["hl"][rows, :], sc["cc"][rows, :], p)
            for s in range(N_SLAB):
                for c in range(rows.start // SCAN_LEN, rows.stop // SCAN_LEN):
                    src = slice(c * SCAN_LEN - rows.start, (c + 1) * SCAN_LEN - rows.start)
                    dst = slice(c * SCAN_PITCH, c * SCAN_PITCH + SCAN_LEN)
                    sc["a_s"][s, dst, :] = a[src, _slab(s)]
                    sc["b_s"][s, dst, :] = bterm[src, _slab(s)]

    for s in range(N_SLAB):
        @add(300)
        def _(s=s):
            a_s, b_s = sc["a_s"], sc["b_s"]
            hv = jnp.zeros((SCAN_CHUNKS, LANES), jnp.float32)
            pv = jnp.ones((SCAN_CHUNKS, LANES), jnp.float32)
            for j in range(SCAN_LEN):
                idx = pl.ds(j, SCAN_CHUNKS, stride=SCAN_PITCH)
                av = a_s[s, idx, :]
                hv = av * hv + b_s[s, idx, :]
                pv = av * pv
                b_s[s, idx, :] = hv
                a_s[s, idx, :] = pv
            cur = sc["carry"][0:1, _slab(s)]
            for c in range(SCAN_CHUNKS):
                src = slice(c * SCAN_PITCH, c * SCAN_PITCH + SCAN_LEN)
                rows = slice(c * SCAN_LEN, (c + 1) * SCAN_LEN)
                sc["hl"][rows, _slab(s)] = b_s[s, src, :] + a_s[s, src, :] * cur
                cur = pv[c:c + 1, :] * cur + hv[c:c + 1, :]
            sc["carry"][:, _slab(s)] = jnp.broadcast_to(cur, (SUBLANES, LANES))

    @add(1024)
    def _():
        sc["xc"][...] = _dot(sc["u"][...], w_in[:, D_LRU:2 * D_LRU])

    for rows in row_pieces:
        @add(500)
        def _(rows=rows):
            y_lru = sc["hl"][rows, :] * jax.nn.gelu(sc["xc"][rows, :])
            sc["y"][rows, 0:D_LRU] = _rmsnorm(y_lru, q["on_lru_g"][...]).astype(jnp.bfloat16)

    for s in range(N_SLAB):
        for r0 in range(0, TM, CONV_ROWS):
            @add(170)
            def _(s=s, r0=r0):
                _dwconv_chunk(sc["cbuf"], q["cc_conv_w"], q["cc_conv_b"], sc["cc"], s, r0,
                              CC_CONV_W, C_HALO, CONV_ROWS)

    for rows in row_pieces:
        @add(600)
        def _(rows=rows):
            cc = sc["cc"][rows, :]
            mu = jnp.mean(cc, axis=-1, keepdims=True)
            cen = cc - mu
            var = jnp.mean(cen * cen, axis=-1, keepdims=True)
            ln = cen * lax.rsqrt(var + EPS) * q["ln_g"][...] + q["ln_b"][...]
            y_cc = ln * jax.nn.sigmoid(ln)
            sc["y"][rows, D_LRU:D_LRU + D_CC] = _rmsnorm(y_cc, q["on_cc_g"][...]).astype(jnp.bfloat16)

    for n0 in range(0, D_MODEL, PIECE_N):
        @add(1150)
        def _(cols=slice(n0, n0 + PIECE_N)):
            sc["h1_next"][:, cols] = x_ref[:, cols] + _dot(sc["y"][...], q["w_out"][:, cols])

    return pieces


def _mlp_pieces(out_ref, q, sc):
    pieces = []
    add = lambda cost: (lambda fn: pieces.append((cost, fn)))
    n_chunks = D_FF // FF_CHUNK
    row_pieces = [slice(r, r + ROW_PIECE) for r in range(0, TM, ROW_PIECE)]

    @add(600)
    def _():
        sc["m"][...] = _rmsnorm(sc["h1"][...], q["g_mlp"][...]).astype(jnp.bfloat16)

    def up(k, n0):
        @add(1100)
        def _():
            cols = slice(k * FF_CHUNK + n0, k * FF_CHUNK + n0 + PIECE_N)
            hid = jnp.maximum(_dot(sc["m"][...], q["w_up"][:, cols]), 0.0)
            sc["hid"][k % 2, :, n0:n0 + PIECE_N] = (hid * hid).astype(jnp.bfloat16)

    def down(k, n0):
        @add(1050)
        def _():
            cols = slice(n0, n0 + PIECE_N)
            base = sc["h1"] if k == 0 else sc["acc"]
            sc["acc"][:, cols] = base[:, cols] + _dot(
                sc["hid"][k % 2], q["w_down"][k * FF_CHUNK:(k + 1) * FF_CHUNK, cols])

    for k in range(n_chunks + 1):
        ups = [functools.partial(up, k, n0) for n0 in range(0, FF_CHUNK, PIECE_N)] if k < n_chunks else []
        downs = [functools.partial(down, k - 1, n0) for n0 in range(0, D_MODEL, PIECE_N)] if k > 0 else []
        for i in range(max(len(ups), len(downs))):
            if i < len(ups):
                ups[i]()
            if i < len(downs):
                downs[i]()

    for rows in row_pieces:
        @add(150)
        def _(rows=rows):
            out_ref[rows, :] = _rmsnorm(sc["acc"][rows, :], q["g_final"][...])

    return pieces


def _emit_interleaved(streams):
    totals = [sum(c for c, _ in s) for s in streams]
    done = [0] * len(streams)
    pos = [0] * len(streams)
    while any(pos[i] < len(s) for i, s in enumerate(streams)):
        live = [i for i, s in enumerate(streams) if pos[i] < len(s)]
        i = min(live, key=lambda j: done[j] / totals[j])
        cost, fn = streams[i][pos[i]]
        fn()
        done[i] += cost
        pos[i] += 1


def _block_kernel(x_ref, meta_ref, g_mix, w_in, lru_conv_w, lru_conv_b, wa, ba, wx, bx, a_param,
                  cc_conv_w, cc_conv_b, ln_g, ln_b, on_lru_g, on_cc_g, w_out,
                  g_mlp, w_up, w_down, g_final,
                  out_ref,
                  xbuf, cbuf, a_s, b_s, xc_ref, hl_ref, cc_ref, carry_ref, xinit, cinit, hinit,
                  h1_ref, u_ref, m_ref, y_ref, hid_ref, acc_ref,
                  *, tiles_per_seq):
    p = dict(g_mix=g_mix, w_in=w_in, lru_conv_w=lru_conv_w, lru_conv_b=lru_conv_b,
             wa=wa, ba=ba, wx=wx, bx=bx, a_param=a_param)
    q = dict(cc_conv_w=cc_conv_w, cc_conv_b=cc_conv_b, ln_g=ln_g, ln_b=ln_b, on_lru_g=on_lru_g,
             on_cc_g=on_cc_g, w_out=w_out, g_mlp=g_mlp, w_up=w_up, w_down=w_down, g_final=g_final)
    step = pl.program_id(0)

    @pl.when(step == 0)
    def _():
        h1_ref[...] = jnp.zeros((TM, D_MODEL), jnp.float32)
        _meta_state(meta_ref, p, xbuf, cbuf, xc_ref, xinit, cinit, hinit)

    @pl.when(step % tiles_per_seq == 0)
    def _():
        for s in range(N_SLAB):
            xbuf[s, 0:X_HALO, :] = xinit[s]
            cbuf[s, 0:C_HALO, :] = cinit[s]
        carry_ref[...] = hinit[...]

    sc = dict(xbuf=xbuf, cbuf=cbuf, a_s=a_s, b_s=b_s, xc=xc_ref, hl=hl_ref, cc=cc_ref,
              carry=carry_ref, u=u_ref, m=m_ref, y=y_ref, hid=hid_ref, acc=acc_ref,
              h1=h1_ref, h1_next=h1_ref)
    _emit_interleaved([_mlp_pieces(out_ref, q, sc), _mixer_pieces(x_ref, p, q, sc)])
    for s in range(N_SLAB):
        xbuf[s, 0:X_HALO, :] = xbuf[s, TM:TM + X_HALO, :]
        cbuf[s, 0:C_HALO, :] = cbuf[s, TM:TM + C_HALO, :]


def _const_spec(shape):
    zeros = (0,) * len(shape)
    return pl.BlockSpec(shape, lambda *_: zeros, pipeline_mode=pl.Buffered(1))


def _block_diag(w):
    per = MXU_DIM // LRU_HEAD_DIM
    w4 = w.reshape(GATE_GROUPS, per, LRU_HEAD_DIM, LRU_HEAD_DIM)
    eye = jnp.eye(per, dtype=w.dtype)
    bd = jnp.einsum("ghij,hk->ghikj", w4, eye)
    return bd.reshape(GATE_GROUPS, MXU_DIM, MXU_DIM).astype(jnp.bfloat16)


def kernel(x, meta_tokens, norm_mix_g, w_in, lru_conv_w, lru_conv_b, lru_gate_a_w, lru_gate_a_b,
           lru_gate_x_w, lru_gate_x_b, lru_a_param, cc_conv_w, cc_conv_b, cc_ln_g, cc_ln_b,
           out_norm_lru_g, out_norm_cc_g, w_out, norm_mlp_g, w_mlp_up, w_mlp_down, norm_final_g):
    batch, seq, d = x.shape
    assert d == D_MODEL and seq % TM == 0
    assert w_in.shape[0] == 1, "single layer"
    bf16 = jnp.bfloat16
    f32 = jnp.float32
    row = lambda v: v.reshape(1, -1)
    tiles_per_seq = seq // TM
    n_tiles = batch * tiles_per_seq

    inputs = (
        x, meta_tokens, row(norm_mix_g[0]), w_in[0].astype(bf16),
        lru_conv_w[0], row(lru_conv_b[0]),
        _block_diag(lru_gate_a_w[0]), row(lru_gate_a_b[0]),
        _block_diag(lru_gate_x_w[0]), row(lru_gate_x_b[0]),
        row(lru_a_param[0]), cc_conv_w[0], row(cc_conv_b[0]),
        row(cc_ln_g[0]), row(cc_ln_b[0]), row(out_norm_lru_g[0]), row(out_norm_cc_g[0]),
        w_out[0].astype(bf16),
        row(norm_mlp_g[0]), w_mlp_up[0].astype(bf16), w_mlp_down[0].astype(bf16), row(norm_final_g),
    )

    def x_index(i):
        j = jnp.minimum(i, n_tiles - 1)
        return (j // tiles_per_seq, j % tiles_per_seq, 0)

    def out_index(i):
        j = jnp.maximum(i - 1, 0)
        return (j // tiles_per_seq, j % tiles_per_seq, 0)

    in_specs = [pl.BlockSpec((None, TM, D_MODEL), x_index)] + [_const_spec(v.shape) for v in inputs[1:]]
    return pl.pallas_call(
        functools.partial(_block_kernel, tiles_per_seq=tiles_per_seq),
        out_shape=jax.ShapeDtypeStruct((batch, seq, D_MODEL), f32),
        grid=(n_tiles + 1,),
        in_specs=in_specs,
        out_specs=pl.BlockSpec((None, TM, D_MODEL), out_index),
        scratch_shapes=[
            pltpu.VMEM((N_SLAB, X_HALO + TM, LANES), f32),
            pltpu.VMEM((N_SLAB, C_HALO + TM, LANES), f32),
            pltpu.VMEM((N_SLAB, SCAN_CHUNKS * SCAN_PITCH, LANES), f32),
            pltpu.VMEM((N_SLAB, SCAN_CHUNKS * SCAN_PITCH, LANES), f32),
            pltpu.VMEM((TM, D_LRU), f32),
            pltpu.VMEM((TM, D_LRU), f32),
            pltpu.VMEM((TM, D_CC), f32),
            pltpu.VMEM((SUBLANES, D_LRU), f32),
            pltpu.VMEM((N_SLAB, X_HALO, LANES), f32),
            pltpu.VMEM((N_SLAB, C_HALO, LANES), f32),
            pltpu.VMEM((SUBLANES, D_LRU), f32),
            pltpu.VMEM((TM, D_MODEL), f32),
            pltpu.VMEM((TM, D_MODEL), bf16),
            pltpu.VMEM((TM, D_MODEL), bf16),
            pltpu.VMEM((TM, D_MODEL), bf16),
            pltpu.VMEM((2, TM, FF_CHUNK), bf16),
            pltpu.VMEM((TM, D_MODEL), f32),
        ],
        compiler_params=pltpu.CompilerParams(
            dimension_semantics=("arbitrary",), vmem_limit_bytes=VMEM_LIMIT),
        name="hybrid_block",
    )(*inputs)
```
